```python
import jax, jax.numpy as jnp
from jax import lax
import numpy as np

D_MODEL = 4096
BATCH = 1
SEQ = 16384
DEPTH = 4

D_FF = 4 * D_MODEL
PLE_DIM = 256
PLE_GATE_RANK = 256
D_CONV_A = D_MODEL // 2
A_GROUP_DIM = 128
N_GROUPS_A = D_CONV_A // A_GROUP_DIM
CONV_A_WIDTH = 31
D_GMLP_B = D_MODEL // 2
B_HEAD_DIM = 128
N_HEADS_B = D_GMLP_B // B_HEAD_DIM
CHUNK = 128
D_CONV_C = D_MODEL
C_GROUP_DIM = 128
N_GROUPS_C = D_CONV_C // C_GROUP_DIM
CONV_C_WIDTH = 3
N_EVEN = (DEPTH + 1) // 2
N_ODD = DEPTH // 2
EPS = 1e-6

kernel_name = "hybrid_convmodule_gmlp_shortconv_trunk"


def rms_norm(x, g):
    xf = x.astype(jnp.float32)
    y = xf * lax.rsqrt(jnp.mean(xf * xf, axis=-1, keepdims=True) + EPS)
    return (y * g.astype(jnp.float32)).astype(x.dtype)


def layer_norm(x, g, b):
    xf = x.astype(jnp.float32)
    mu = jnp.mean(xf, axis=-1, keepdims=True)
    xc = xf - mu
    var = jnp.mean(xc * xc, axis=-1, keepdims=True)
    y = xc * lax.rsqrt(var + EPS) * g.astype(jnp.float32) + b.astype(jnp.float32)
    return y.astype(x.dtype)


def causal_depthwise_conv(x, w):
    K, C = w.shape
    return lax.conv_general_dilated(
        x, w[:, None, :].astype(x.dtype), window_strides=(1,),
        padding=[(K - 1, 0)], dimension_numbers=("NWC", "WIO", "NWC"),
        feature_group_count=C)


def conv_gmlp_mixer(h, w_in, a_conv_w, a_conv_b, a_ln_g, a_ln_b,
                    b_ln_g, b_ln_b, b_ws, b_bs, w_out):
    z = h @ w_in
    a_val, a_gate, u, v = jnp.split(
        z, [D_CONV_A, 2 * D_CONV_A, 2 * D_CONV_A + D_GMLP_B], axis=-1)
    a = a_val * jax.nn.sigmoid(a_gate)
    a = causal_depthwise_conv(a, a_conv_w) + a_conv_b
    a = jax.nn.silu(layer_norm(a, a_ln_g, a_ln_b))
    u = jax.nn.gelu(u)
    v = layer_norm(jax.nn.gelu(v), b_ln_g, b_ln_b)
    bsz, t_len, _ = v.shape
    vc = v.reshape(bsz, t_len // CHUNK, CHUNK, N_HEADS_B, B_HEAD_DIM)
    causal = jnp.tril(jnp.ones((CHUNK, CHUNK), dtype=bool))
    ws = jnp.where(causal[None], b_ws, 0).astype(v.dtype)
    sv = jnp.einsum("hts,bcshd->bcthd", ws, vc) + b_bs.T[:, :, None].astype(v.dtype)
    b = u * sv.reshape(bsz, t_len, D_GMLP_B)
    return jnp.concatenate([a, b], axis=-1) @ w_out


def short_conv_mixer(h, w_in, conv_w, w_out):
    gate_b, gate_c, xv = jnp.split(h @ w_in, 3, axis=-1)
    y = gate_b * causal_depthwise_conv(gate_c * xv, conv_w)
    return y @ w_out


def sq_relu_mlp(h, w1, w2):
    return jnp.square(jax.nn.relu(h @ w1)) @ w2


def setup_inputs(seed: int = 0) -> dict:
    key = jax.random.key(seed)
    ks = jax.random.split(key, 24)
    f32 = jnp.float32

    def nrm(k, shape, scale):
        return jax.random.normal(k, shape, f32) * scale

    def gain(k, shape):
        return 1.0 + 0.05 * jax.random.normal(k, shape, f32)

    d_in_even = 2 * D_CONV_A + 2 * D_GMLP_B
    return {
        "x": nrm(ks[0], (BATCH, SEQ, D_MODEL), 1.0),
        "p": nrm(ks[1], (DEPTH, BATCH, SEQ, PLE_DIM), 1.0),
        "norm_mix_g": gain(ks[2], (DEPTH, D_MODEL)),
        "norm_ffn_g": gain(ks[3], (DEPTH, D_MODEL)),
        "norm_ple_g": gain(ks[4], (DEPTH, D_MODEL)),
        "final_norm_g": gain(ks[5], (D_MODEL,)),
        "even_w_in": nrm(ks[6], (N_EVEN, D_MODEL, d_in_even), D_MODEL ** -0.5),
        "even_w_out": nrm(ks[7], (N_EVEN, D_CONV_A + D_GMLP_B, D_MODEL),
                          (D_CONV_A + D_GMLP_B) ** -0.5),
        "a_conv_w": nrm(ks[8], (N_EVEN, CONV_A_WIDTH, D_CONV_A), CONV_A_WIDTH ** -0.5),
        "a_conv_b": nrm(ks[9], (N_EVEN, D_CONV_A), 0.02),
        "a_ln_g": gain(ks[10], (N_EVEN, D_CONV_A)),
        "a_ln_b": nrm(ks[11], (N_EVEN, D_CONV_A), 0.02),
        "b_ln_g": gain(ks[12], (N_EVEN, D_GMLP_B)),
        "b_ln_b": nrm(ks[13], (N_EVEN, D_GMLP_B), 0.02),
        "b_ws": nrm(ks[14], (N_EVEN, N_HEADS_B, CHUNK, CHUNK), CHUNK ** -0.5),
        "b_bs": gain(ks[15], (N_EVEN, N_HEADS_B, CHUNK)),
        "odd_w_in": nrm(ks[16], (N_ODD, D_MODEL, 3 * D_CONV_C), D_MODEL ** -0.5),
        "odd_conv_w": nrm(ks[17], (N_ODD, CONV_C_WIDTH, D_CONV_C), CONV_C_WIDTH ** -0.5),
        "odd_w_out": nrm(ks[18], (N_ODD, D_CONV_C, D_MODEL), D_CONV_C ** -0.5),
        "ffn_w1": nrm(ks[19], (DEPTH, D_MODEL, D_FF), D_MODEL ** -0.5),
        "ffn_w2": nrm(ks[20], (DEPTH, D_FF, D_MODEL), D_FF ** -0.5),
        "ple_proj": nrm(ks[21], (DEPTH, PLE_DIM, D_MODEL), PLE_DIM ** -0.5),
        "ple_gate_down": nrm(ks[22], (DEPTH, D_MODEL, PLE_GATE_RANK), D_MODEL ** -0.5),
        "ple_gate_up": nrm(ks[23], (DEPTH, PLE_GATE_RANK, D_MODEL), PLE_GATE_RANK ** -0.5),
    }


def reference(x, p, norm_mix_g, norm_ffn_g, norm_ple_g, final_norm_g,
              even_w_in, even_w_out, a_conv_w, a_conv_b, a_ln_g, a_ln_b,
              b_ln_g, b_ln_b, b_ws, b_bs, odd_w_in, odd_conv_w, odd_w_out,
              ffn_w1, ffn_w2, ple_proj, ple_gate_down, ple_gate_up):
    h = x
    for i in range(DEPTH):
        hn = rms_norm(h, norm_mix_g[i])
        if i % 2 == 0:
            e = i // 2
            h = h + conv_gmlp_mixer(hn, even_w_in[e], a_conv_w[e], a_conv_b[e],
                                    a_ln_g[e], a_ln_b[e], b_ln_g[e], b_ln_b[e],
                                    b_ws[e], b_bs[e], even_w_out[e])
        else:
            o = i // 2
            h = h + short_conv_mixer(hn, odd_w_in[o], odd_conv_w[o], odd_w_out[o])
        h = h + sq_relu_mlp(rms_norm(h, norm_ffn_g[i]), ffn_w1[i], ffn_w2[i])
        gate = jax.nn.sigmoid((rms_norm(h, norm_ple_g[i]) @ ple_gate_down[i]) @ ple_gate_up[i])
        h = h + gate * (p[i] @ ple_proj[i])
    return rms_norm(h, final_norm_g)
```

```python
import functools
import math

import jax
import jax.numpy as jnp
from jax import lax
from jax.experimental import pallas as pl
from jax.experimental.pallas import tpu as pltpu

EPS = 1e-6
CHUNK = 128
HEAD = 128
HALO = 32
CONV_A_WIDTH = 31
CONV_C_WIDTH = 3
GELU_C = math.sqrt(2.0 / math.pi)

V7X_VMEM_BYTES = 64 * 1024 * 1024
VMEM_LIMIT = V7X_VMEM_BYTES - 8 * 1024 * 1024

F32 = jnp.float32
BF16 = jnp.bfloat16


def _params(semantics):
    return pltpu.CompilerParams(dimension_semantics=semantics,
                                vmem_limit_bytes=VMEM_LIMIT)


def _rms_scale(x):
    return x * lax.rsqrt(jnp.mean(x * x, axis=-1, keepdims=True) + EPS)


def _layer_norm(x, g, b):
    mu = jnp.mean(x, axis=-1, keepdims=True)
    xc = x - mu
    var = jnp.mean(xc * xc, axis=-1, keepdims=True)
    return xc * lax.rsqrt(var + EPS) * g + b


def _gelu(x):
    return 0.5 * x * (1.0 + jnp.tanh(GELU_C * (x + 0.044715 * (x * x * x))))


def _rmsnorm_kernel(x_ref, g_ref, o_ref):
    o_ref[...] = (_rms_scale(x_ref[...]) * g_ref[...]).astype(o_ref.dtype)


def rmsnorm_bf16(x, g, *, tm=256):
    m, d = x.shape
    return pl.pallas_call(
        _rmsnorm_kernel,
        grid=(m // tm,),
        in_specs=[pl.BlockSpec((tm, d), lambda i: (i, 0)),
                  pl.BlockSpec((1, d), lambda i: (0, 0))],
        out_specs=pl.BlockSpec((tm, d), lambda i: (i, 0)),
        out_shape=jax.ShapeDtypeStruct((m, d), BF16),
        compiler_params=_params(("parallel",)),
        name="rmsnorm_bf16",
    )(x, g.reshape(1, d))


def _mm_kernel(*refs, nk, act, has_resid):
    if has_resid:
        x_ref, w_ref, r_ref, o_ref = refs[:4]
        rest = refs[4:]
    else:
        x_ref, w_ref, o_ref = refs[:3]
        r_ref = None
        rest = refs[3:]

    def finish(acc):
        if act == "relu2":
            acc = jnp.square(jnp.maximum(acc, 0.0))
        if has_resid:
            acc = r_ref[...] + acc
        o_ref[...] = acc.astype(o_ref.dtype)

    prod = jnp.dot(x_ref[...], w_ref[...], preferred_element_type=F32)
    if nk == 1:
        finish(prod)
        return

    acc_ref, = rest
    k = pl.program_id(2)

    @pl.when(k == 0)
    def _():
        acc_ref[...] = prod

    @pl.when(jnp.logical_and(k > 0, k < nk - 1))
    def _():
        acc_ref[...] += prod

    @pl.when(k == nk - 1)
    def _():
        finish(acc_ref[...] + prod)


def matmul(x, w, *, resid=None, act=None, out_dtype=BF16, tm=1024, tn=1024, tk=None, name):
    m, kdim = x.shape
    n = w.shape[1]
    tk = kdim if tk is None else tk
    nk = kdim // tk
    assert m % tm == 0 and n % tn == 0 and kdim % tk == 0
    in_specs = [pl.BlockSpec((tm, tk), lambda i, j, k: (i, k)),
                pl.BlockSpec((tk, tn), lambda i, j, k: (k, j))]
    args = [x, w]
    if resid is not None:
        in_specs.append(pl.BlockSpec((tm, tn), lambda i, j, k: (i, j)))
        args.append(resid)
    scratch = [pltpu.VMEM((tm, tn), F32)] if nk > 1 else []
    return pl.pallas_call(
        functools.partial(_mm_kernel, nk=nk, act=act, has_resid=resid is not None),
        grid=(m // tm, n // tn, nk),
        in_specs=in_specs,
        out_specs=pl.BlockSpec((tm, tn), lambda i, j, k: (i, j)),
        out_shape=jax.ShapeDtypeStruct((m, n), out_dtype),
        scratch_shapes=scratch,
        compiler_params=_params(("parallel", "parallel", "arbitrary")),
        name=name,
    )(*args)


def _even_mixer_kernel(aval_ref, agate_ref, u_ref, v_ref, hval_ref, hgate_ref,
                       cw_ref, cb_ref, alg_ref, alb_ref, blg_ref, blb_ref,
                       ws_ref, bias_ref, o_ref, glu_ref, conv_ref, vn_ref, *, tm, dc):
    i = pl.program_id(0)

    def glu(val_ref, gate_ref):
        return val_ref[...].astype(F32) * jax.nn.sigmoid(gate_ref[...].astype(F32))

    halo = glu(hval_ref, hgate_ref)
    glu_ref[0:HALO, :] = jnp.where(i > 0, halo, 0.0)
    glu_ref[HALO:, :] = glu(aval_ref, agate_ref)

    rb = 64
    base = HALO - (CONV_A_WIDTH - 1)

    def conv_cols(c, carry):
        cs = pl.ds(pl.multiple_of(c * HEAD, HEAD), HEAD)
        wk = cw_ref[:, cs]
        for r in range(tm // rb):
            acc = jnp.zeros((rb, HEAD), F32)
            for k in range(CONV_A_WIDTH):
                acc = acc + wk[k:k + 1, :] * glu_ref[r * rb + base + k: r * rb + base + k + rb, cs]
            conv_ref[r * rb:(r + 1) * rb, cs] = acc
        return carry

    lax.fori_loop(0, dc // HEAD, conv_cols, 0)

    a = conv_ref[...] + cb_ref[...]
    a = _layer_norm(a, alg_ref[...], alb_ref[...])
    o_ref[:, 0:dc] = (a * jax.nn.sigmoid(a)).astype(o_ref.dtype)

    gv = _gelu(v_ref[...].astype(F32))
    vn_ref[...] = _layer_norm(gv, blg_ref[...], blb_ref[...]).astype(vn_ref.dtype)

    nc = tm // CHUNK
    row = lax.broadcasted_iota(jnp.int32, (CHUNK, CHUNK), 0)
    col = lax.broadcasted_iota(jnp.int32, (CHUNK, CHUNK), 1)
    causal = row >= col
    for h in range(dc // HEAD):
        hs = slice(h * HEAD, (h + 1) * HEAD)
        ws = jnp.where(causal, ws_ref[h], 0.0).astype(BF16)
        vcat = jnp.concatenate(
            [vn_ref[c * CHUNK:(c + 1) * CHUNK, hs] for c in range(nc)], axis=1)
        sv = jnp.dot(ws, vcat, preferred_element_type=F32)
        bias = bias_ref[:, hs]
        for c in range(nc):
            rs = slice(c * CHUNK, (c + 1) * CHUNK)
            u = _gelu(u_ref[rs, hs].astype(F32))
            o_ref[rs, dc + h * HEAD: dc + (h + 1) * HEAD] = (
                u * (sv[:, c * HEAD:(c + 1) * HEAD] + bias)).astype(o_ref.dtype)


def even_mixer(z, conv_w, conv_b, a_ln_g, a_ln_b, b_ln_g, b_ln_b, b_ws, b_bs, *, tm=256):
    m = z.shape[0]
    dc = z.shape[1] // 4
    nh = dc // HEAD
    hb = tm // HALO
    cw = jnp.concatenate([conv_w, jnp.zeros((1, dc), F32)], axis=0)
    bias = jnp.repeat(b_bs.T, HEAD, axis=1)
    row = lambda v: v.reshape(1, dc)

    def col_spec(cidx):
        return pl.BlockSpec((tm, dc), lambda i: (i, cidx))

    def halo_spec(cidx):
        return pl.BlockSpec((HALO, dc), lambda i: (jnp.maximum(i * hb - 1, 0), cidx))

    const = lambda shape: pl.BlockSpec(shape, lambda i: (0,) * len(shape))
    return pl.pallas_call(
        functools.partial(_even_mixer_kernel, tm=tm, dc=dc),
        grid=(m // tm,),
        in_specs=[col_spec(0), col_spec(1), col_spec(2), col_spec(3),
                  halo_spec(0), halo_spec(1),
                  const((CONV_A_WIDTH + 1, dc)), const((1, dc)), const((1, dc)), const((1, dc)),
                  const((1, dc)), const((1, dc)),
                  const((nh, CHUNK, CHUNK)), const((CHUNK, dc))],
        out_specs=pl.BlockSpec((tm, 2 * dc), lambda i: (i, 0)),
        out_shape=jax.ShapeDtypeStruct((m, 2 * dc), BF16),
        scratch_shapes=[pltpu.VMEM((HALO + tm, dc), F32),
                        pltpu.VMEM((tm, dc), F32),
                        pltpu.VMEM((tm, dc), BF16)],
        compiler_params=_params(("parallel",)),
        name="even_mixer",
    )(z, z, z, z, z, z, cw, row(conv_b), row(a_ln_g), row(a_ln_b), row(b_ln_g), row(b_ln_b),
      b_ws, bias)


def _odd_mixer_kernel(gb_ref, gc_ref, xv_ref, hgc_ref, hxv_ref, cw_ref, o_ref, g_ref, *, tm):
    i = pl.program_id(0)
    halo = hgc_ref[...].astype(F32) * hxv_ref[...].astype(F32)
    g_ref[0:HALO, :] = jnp.where(i > 0, halo, 0.0)
    g_ref[HALO:, :] = gc_ref[...].astype(F32) * xv_ref[...].astype(F32)
    base = HALO - (CONV_C_WIDTH - 1)
    acc = cw_ref[0:1, :] * g_ref[base:base + tm, :]
    for k in range(1, CONV_C_WIDTH):
        acc = acc + cw_ref[k:k + 1, :] * g_ref[base + k:base + k + tm, :]
    o_ref[...] = (gb_ref[...].astype(F32) * acc).astype(o_ref.dtype)


def odd_mixer(z, conv_w, *, tm=256):
    m = z.shape[0]
    dc = z.shape[1] // 3
    hb = tm // HALO
    cw = jnp.concatenate([conv_w, jnp.zeros((8 - CONV_C_WIDTH, dc), F32)], axis=0)
    col_spec = lambda cidx: pl.BlockSpec((tm, dc), lambda i: (i, cidx))
    halo_spec = lambda cidx: pl.BlockSpec((HALO, dc), lambda i: (jnp.maximum(i * hb - 1, 0), cidx))
    return pl.pallas_call(
        functools.partial(_odd_mixer_kernel, tm=tm),
        grid=(m // tm,),
        in_specs=[col_spec(0), col_spec(1), col_spec(2), halo_spec(1), halo_spec(2),
                  pl.BlockSpec((8, dc), lambda i: (0, 0))],
        out_specs=pl.BlockSpec((tm, dc), lambda i: (i, 0)),
        out_shape=jax.ShapeDtypeStruct((m, dc), BF16),
        scratch_shapes=[pltpu.VMEM((HALO + tm, dc), F32)],
        compiler_params=_params(("parallel",)),
        name="odd_mixer",
    )(z, z, z, z, z, cw)


def _ple_kernel(h_ref, p_ref, gple_ref, gd_ref, gu_ref, proj_ref, gnext_ref, *out_refs, last):
    h = h_ref[...]
    hn = (_rms_scale(h) * gple_ref[...]).astype(BF16)
    t = jnp.dot(hn, gd_ref[...], preferred_element_type=F32)
    gate = jax.nn.sigmoid(jnp.dot(t.astype(BF16), gu_ref[...], preferred_element_type=F32))
    pp = jnp.dot(p_ref[...].astype(BF16), proj_ref[...], preferred_element_type=F32)
    hnew = h + gate * pp
    nxt = _rms_scale(hnew) * gnext_ref[...]
    if last:
        out_refs[0][...] = nxt
    else:
        out_refs[0][...] = hnew
        out_refs[1][...] = nxt.astype(BF16)


def ple_layer(h, p, g_ple, gate_down, gate_up, proj, g_next, *, last, tm=256):
    m, d = h.shape
    pd = p.shape[1]
    r = gate_down.shape[1]
    row_spec = pl.BlockSpec((tm, d), lambda i: (i, 0))
    const = lambda shape: pl.BlockSpec(shape, lambda i: (0, 0))
    if last:
        out_shape = jax.ShapeDtypeStruct((m, d), F32)
        out_specs = row_spec
    else:
        out_shape = (jax.ShapeDtypeStruct((m, d), F32), jax.ShapeDtypeStruct((m, d), BF16))
        out_specs = (row_spec, row_spec)
    return pl.pallas_call(
        functools.partial(_ple_kernel, last=last),
        grid=(m // tm,),
        in_specs=[row_spec, pl.BlockSpec((tm, pd), lambda i: (i, 0)), const((1, d)),
                  const((d, r)), const((r, d)), const((pd, d)), const((1, d))],
        out_specs=out_specs,
        out_shape=out_shape,
        compiler_params=_params(("parallel",)),
        name="ple_last" if last else "ple",
    )(h, p, g_ple.reshape(1, d), gate_down, gate_up, proj, g_next.reshape(1, d))


def kernel(x, p, norm_mix_g, norm_ffn_g, norm_ple_g, final_norm_g, even_w_in, even_w_out, a_conv_w, a_conv_b, a_ln_g, a_ln_b, b_ln_g, b_ln_b, b_ws, b_bs, odd_w_in, odd_conv_w, odd_w_out, ffn_w1, ffn_w2, ple_proj, ple_gate_down, ple_gate_up):
    bsz, seq, d = x.shape
    depth = p.shape[0]
    assert bsz == 1
    h = x.reshape(seq, d)
    bf = lambda w: w.astype(BF16)

    hn = rmsnorm_bf16(h, norm_mix_g[0])
    out = None
    for i in range(depth):
        if i % 2 == 0:
            e = i // 2
            z = matmul(hn, bf(even_w_in[e]), name="even_in")
            y = even_mixer(z, a_conv_w[e], a_conv_b[e], a_ln_g[e], a_ln_b[e],
                           b_ln_g[e], b_ln_b[e], b_ws[e], b_bs[e])
            h = matmul(y, bf(even_w_out[e]), resid=h, out_dtype=F32, name="even_out")
        else:
            o = i // 2
            z = matmul(hn, bf(odd_w_in[o]), name="odd_in")
            y = odd_mixer(z, odd_conv_w[o])
            h = matmul(y, bf(odd_w_out[o]), resid=h, out_dtype=F32, name="odd_out")
        hf = rmsnorm_bf16(h, norm_ffn_g[i])
        hid = matmul(hf, bf(ffn_w1[i]), act="relu2", name="ffn_up")
        h = matmul(hid, bf(ffn_w2[i]), resid=h, out_dtype=F32, tk=2048, name="ffn_down")
        last = i == depth - 1
        g_next = final_norm_g if last else norm_mix_g[i + 1]
        res = ple_layer(h, p[i].reshape(seq, -1), norm_ple_g[i], bf(ple_gate_down[i]),
                        bf(ple_gate_up[i]), bf(ple_proj[i]), g_next, last=last)
        if last:
            out = res
        else:
            h, hn = res
    return out.reshape(bsz, seq, d)
```

```python
import functools
import math

import jax
import jax.numpy as jnp
from jax import lax
from jax.experimental import pallas as pl
from jax.experimental.pallas import tpu as pltpu

EPS = 1e-6
CHUNK = 128
HEAD = 128
HALO = 32
SUBLANES = 8
BF16_ROWS = 16
GELU_C = math.sqrt(2.0 / math.pi)

V7X_VMEM_BYTES = 64 * 1024 * 1024
VMEM_LIMIT = V7X_VMEM_BYTES - 3 * 1024 * 1024

F32 = jnp.float32
BF16 = jnp.bfloat16


def _params(semantics):
    return pltpu.CompilerParams(dimension_semantics=semantics,
                                vmem_limit_bytes=VMEM_LIMIT)


def _rms_scale(x):
    return x * lax.rsqrt(jnp.mean(x * x, axis=-1, keepdims=True) + EPS)


def _layer_norm(x, g, b):
    mu = jnp.mean(x, axis=-1, keepdims=True)
    xc = x - mu
    var = jnp.mean(xc * xc, axis=-1, keepdims=True)
    return xc * lax.rsqrt(var + EPS) * g + b


def _gelu(x):
    return 0.5 * x * (1.0 + jnp.tanh(GELU_C * (x + 0.044715 * (x * x * x))))


def _rmsnorm_kernel(x_ref, g_ref, o_ref):
    o_ref[...] = (_rms_scale(x_ref[...]) * g_ref[...]).astype(o_ref.dtype)


def rmsnorm_bf16(x, g, *, tm=256):
    m, d = x.shape
    return pl.pallas_call(
        _rmsnorm_kernel,
        grid=(m // tm,),
        in_specs=[pl.BlockSpec((tm, d), lambda i: (i, 0)),
                  pl.BlockSpec((1, d), lambda i: (0, 0))],
        out_specs=pl.BlockSpec((tm, d), lambda i: (i, 0)),
        out_shape=jax.ShapeDtypeStruct((m, d), BF16),
        compiler_params=_params(("arbitrary",)),
        name="rmsnorm_bf16",
    )(x, g.reshape(1, d))


def _mm_kernel(*refs, nk, act, has_resid, has_side):
    refs = list(refs)
    x_ref, w_ref = refs[:2]
    pos = 2
    r_ref = s_ref = so_ref = None
    if has_resid:
        r_ref = refs[pos]
        pos += 1
    if has_side:
        s_ref = refs[pos]
        pos += 1
    o_ref = refs[pos]
    pos += 1
    if has_side:
        so_ref = refs[pos]
        so_ref[...] = s_ref[...].astype(so_ref.dtype)

    def product():
        return jnp.dot(x_ref[...], w_ref[...], preferred_element_type=F32)

    def first(acc):
        if act == "relu2":
            acc = jnp.square(jnp.maximum(acc, 0.0))
        if has_resid:
            acc = r_ref[...] + acc
        return acc.astype(o_ref.dtype)

    if nk == 1:
        o_ref[...] = first(product())
        return

    assert act is None and o_ref.dtype == F32
    k = pl.program_id(2)

    @pl.when(k == 0)
    def _():
        o_ref[...] = first(product())

    @pl.when(k > 0)
    def _():
        o_ref[...] += product()


def _side_blocks(nsteps, rows):
    nblocks = 1
    while nblocks * 2 <= min(nsteps, rows // BF16_ROWS) and rows % (nblocks * 2) == 0:
        nblocks *= 2
    return nblocks


def matmul(x, w, *, resid=None, act=None, out_dtype=BF16, tm=1024, tn=1024, tk=None,
           side=None, name):
    m, kdim = x.shape
    n = w.shape[1]
    tm, tn = min(tm, m), min(tn, n)
    tk = kdim if tk is None else min(tk, kdim)
    nk = kdim // tk
    assert m % tm == 0 and n % tn == 0 and kdim % tk == 0
    gi, gj = m // tm, n // tn
    in_specs = [pl.BlockSpec((tm, tk), lambda i, j, k: (i, k)),
                pl.BlockSpec((tk, tn), lambda i, j, k: (k, j))]
    args = [x, w]
    if resid is not None:
        in_specs.append(pl.BlockSpec((tm, tn), lambda i, j, k: (i, j)))
        args.append(resid)
    out_specs = [pl.BlockSpec((tm, tn), lambda i, j, k: (i, j))]
    out_shape = [jax.ShapeDtypeStruct((m, n), out_dtype)]
    if side is not None:
        stack, layer = side
        _, srows, scols = stack.shape
        nblocks = _side_blocks(gi * gj * nk, srows)
        brow = srows // nblocks

        def side_block(i, j, k):
            return jnp.minimum((i * gj + j) * nk + k, nblocks - 1)

        in_specs.append(pl.BlockSpec((None, brow, scols),
                                     lambda i, j, k: (layer, side_block(i, j, k), 0)))
        args.append(stack)
        out_specs.append(pl.BlockSpec((brow, scols), lambda i, j, k: (side_block(i, j, k), 0)))
        out_shape.append(jax.ShapeDtypeStruct((srows, scols), BF16))
    res = pl.pallas_call(
        functools.partial(_mm_kernel, nk=nk, act=act, has_resid=resid is not None,
                          has_side=side is not None),
        grid=(gi, gj, nk),
        in_specs=in_specs,
        out_specs=out_specs,
        out_shape=out_shape,
        compiler_params=_params(("arbitrary", "arbitrary", "arbitrary")),
        name=name,
    )(*args)
    return (res[0], res[1]) if side is not None else (res[0], None)


def _even_mixer_kernel(aval_ref, agate_ref, u_ref, v_ref, hval_ref, hgate_ref,
                       cw_ref, cb_ref, alg_ref, alb_ref, blg_ref, blb_ref,
                       ws_ref, bias_ref, o_ref, glu_ref, conv_ref, vn_ref, *, tm, dc, width):
    i = pl.program_id(0)

    def glu(val_ref, gate_ref):
        return val_ref[...].astype(F32) * jax.nn.sigmoid(gate_ref[...].astype(F32))

    halo = glu(hval_ref, hgate_ref)
    glu_ref[0:HALO, :] = jnp.where(i > 0, halo, 0.0)
    glu_ref[HALO:, :] = glu(aval_ref, agate_ref)

    rb = 64
    nq = -(-width // SUBLANES)
    assert HALO >= SUBLANES * nq

    def conv_cols(c, carry):
        cs = pl.ds(pl.multiple_of(c * HEAD, HEAD), HEAD)
        wk = cw_ref[:, cs]
        for blk in range(tm // rb):
            r0 = HALO + blk * rb
            y = None
            for r in range(SUBLANES):
                b = None
                for q in range(nq):
                    s = SUBLANES * q + r
                    if s >= width:
                        continue
                    start = r0 - SUBLANES - SUBLANES * q
                    term = wk[s:s + 1, :] * glu_ref[start:start + rb + SUBLANES, cs]
                    b = term if b is None else b + term
                part = b[SUBLANES - r:SUBLANES - r + rb, :]
                y = part if y is None else y + part
            conv_ref[blk * rb:(blk + 1) * rb, cs] = y
        return carry

    lax.fori_loop(0, dc // HEAD, conv_cols, 0)

    a = conv_ref[...] + cb_ref[...]
    a = _layer_norm(a, alg_ref[...], alb_ref[...])
    o_ref[:, 0:dc] = (a * jax.nn.sigmoid(a)).astype(o_ref.dtype)

    gv = _gelu(v_ref[...].astype(F32))
    vn_ref[...] = _layer_norm(gv, blg_ref[...], blb_ref[...]).astype(vn_ref.dtype)

    nc = tm // CHUNK
    row = lax.broadcasted_iota(jnp.int32, (CHUNK, CHUNK), 0)
    col = lax.broadcasted_iota(jnp.int32, (CHUNK, CHUNK), 1)
    causal = row >= col
    for h in range(dc // HEAD):
        hs = slice(h * HEAD, (h + 1) * HEAD)
        ws = jnp.where(causal, ws_ref[h], 0.0).astype(BF16)
        vcat = jnp.concatenate(
            [vn_ref[c * CHUNK:(c + 1) * CHUNK, hs] for c in range(nc)], axis=1)
        sv = jnp.dot(ws, vcat, preferred_element_type=F32)
        bias = bias_ref[:, hs]
        for c in range(nc):
            rs = slice(c * CHUNK, (c + 1) * CHUNK)
            u = _gelu(u_ref[rs, hs].astype(F32))
            o_ref[rs, dc + h * HEAD: dc + (h + 1) * HEAD] = (
                u * (sv[:, c * HEAD:(c + 1) * HEAD] + bias)).astype(o_ref.dtype)


def even_mixer(z, conv_w, conv_b, a_ln_g, a_ln_b, b_ln_g, b_ln_b, b_ws, b_bs, *, tm=256):
    m = z.shape[0]
    dc = z.shape[1] // 4
    nh = dc // HEAD
    tm = min(tm, m)
    hb = tm // HALO
    width = conv_w.shape[0]
    wrows = -(-width // SUBLANES) * SUBLANES
    cw = jnp.concatenate([conv_w[::-1], jnp.zeros((wrows - width, dc), F32)], axis=0)
    bias = jnp.repeat(b_bs.T, HEAD, axis=1)
    row = lambda v: v.reshape(1, dc)

    def col_spec(cidx):
        return pl.BlockSpec((tm, dc), lambda i: (i, cidx))

    def halo_spec(cidx):
        return pl.BlockSpec((HALO, dc), lambda i: (jnp.maximum(i * hb - 1, 0), cidx))

    const = lambda shape: pl.BlockSpec(shape, lambda i: (0,) * len(shape))
    return pl.pallas_call(
        functools.partial(_even_mixer_kernel, tm=tm, dc=dc, width=width),
        grid=(m // tm,),
        in_specs=[col_spec(0), col_spec(1), col_spec(2), col_spec(3),
                  halo_spec(0), halo_spec(1),
                  const((wrows, dc)), const((1, dc)), const((1, dc)), const((1, dc)),
                  const((1, dc)), const((1, dc)),
                  const((nh, CHUNK, CHUNK)), const((CHUNK, dc))],
        out_specs=pl.BlockSpec((tm, 2 * dc), lambda i: (i, 0)),
        out_shape=jax.ShapeDtypeStruct((m, 2 * dc), BF16),
        scratch_shapes=[pltpu.VMEM((HALO + tm, dc), F32),
                        pltpu.VMEM((tm, dc), F32),
                        pltpu.VMEM((tm, dc), BF16)],
        compiler_params=_params(("arbitrary",)),
        name="even_mixer",
    )(z, z, z, z, z, z, cw, row(conv_b), row(a_ln_g), row(a_ln_b), row(b_ln_g), row(b_ln_b),
      b_ws, bias)


def _odd_mixer_kernel(gb_ref, gc_ref, xv_ref, hgc_ref, hxv_ref, cw_ref, o_ref, g_ref, *, tm, width):
    i = pl.program_id(0)
    halo = hgc_ref[...].astype(F32) * hxv_ref[...].astype(F32)
    g_ref[0:HALO, :] = jnp.where(i > 0, halo, 0.0)
    g_ref[HALO:, :] = gc_ref[...].astype(F32) * xv_ref[...].astype(F32)
    base = HALO - (width - 1)
    acc = cw_ref[0:1, :] * g_ref[base:base + tm, :]
    for k in range(1, width):
        acc = acc + cw_ref[k:k + 1, :] * g_ref[base + k:base + k + tm, :]
    o_ref[...] = (gb_ref[...].astype(F32) * acc).astype(o_ref.dtype)


def odd_mixer(z, conv_w, *, tm=256):
    m = z.shape[0]
    dc = z.shape[1] // 3
    tm = min(tm, m)
    hb = tm // HALO
    width = conv_w.shape[0]
    assert width - 1 <= HALO
    wrows = -(-width // SUBLANES) * SUBLANES
    cw = jnp.concatenate([conv_w, jnp.zeros((wrows - width, dc), F32)], axis=0)
    col_spec = lambda cidx: pl.BlockSpec((tm, dc), lambda i: (i, cidx))
    halo_spec = lambda cidx: pl.BlockSpec((HALO, dc), lambda i: (jnp.maximum(i * hb - 1, 0), cidx))
    return pl.pallas_call(
        functools.partial(_odd_mixer_kernel, tm=tm, width=width),
        grid=(m // tm,),
        in_specs=[col_spec(0), col_spec(1), col_spec(2), halo_spec(1), halo_spec(2),
                  pl.BlockSpec((wrows, dc), lambda i: (0, 0))],
        out_specs=pl.BlockSpec((tm, dc), lambda i: (i, 0)),
        out_shape=jax.ShapeDtypeStruct((m, dc), BF16),
        scratch_shapes=[pltpu.VMEM((HALO + tm, dc), F32)],
        compiler_params=_params(("arbitrary",)),
        name="odd_mixer",
    )(z, z, z, z, z, cw)


def _ple_kernel(h_ref, p_ref, gple_ref, gd_ref, gu_ref, proj_ref, gnext_ref, *out_refs, last):
    h = h_ref[...]
    hn = (_rms_scale(h) * gple_ref[...]).astype(BF16)
    t = jnp.dot(hn, gd_ref[...], preferred_element_type=F32)
    gate = jax.nn.sigmoid(jnp.dot(t.astype(BF16), gu_ref[...], preferred_element_type=F32))
    pp = jnp.dot(p_ref[...].astype(BF16), proj_ref[...], preferred_element_type=F32)
    hnew = h + gate * pp
    nxt = _rms_scale(hnew) * gnext_ref[...]
    if last:
        out_refs[0][...] = nxt
    else:
        out_refs[0][...] = hnew
        out_refs[1][...] = nxt.astype(BF16)


def ple_layer(h, p, layer, g_ple, gate_down, gate_up, proj, g_next, *, last, tm=256):
    m, d = h.shape
    pd = p.shape[-1]
    r = gate_down.shape[-1]
    tm = min(tm, m)
    row_spec = pl.BlockSpec((tm, d), lambda i: (i, 0))
    vec = lambda: pl.BlockSpec((1, d), lambda i: (0, 0))
    stacked = lambda a, b: pl.BlockSpec((None, a, b), lambda i: (layer, 0, 0))
    if last:
        out_shape = jax.ShapeDtypeStruct((m, d), F32)
        out_specs = row_spec
    else:
        out_shape = (jax.ShapeDtypeStruct((m, d), F32), jax.ShapeDtypeStruct((m, d), BF16))
        out_specs = (row_spec, row_spec)
    return pl.pallas_call(
        functools.partial(_ple_kernel, last=last),
        grid=(m // tm,),
        in_specs=[row_spec, pl.BlockSpec((None, tm, pd), lambda i: (layer, i, 0)), vec(),
                  stacked(d, r), stacked(r, d), stacked(pd, d), vec()],
        out_specs=out_specs,
        out_shape=out_shape,
        compiler_params=_params(("arbitrary",)),
        name="ple_last" if last else "ple",
    )(h, p, g_ple.reshape(1, d), gate_down, gate_up, proj, g_next.reshape(1, d))


def kernel(x, p, norm_mix_g, norm_ffn_g, norm_ple_g, final_norm_g, even_w_in, even_w_out, a_conv_w, a_conv_b, a_ln_g, a_ln_b, b_ln_g, b_ln_b, b_ws, b_bs, odd_w_in, odd_conv_w, odd_w_out, ffn_w1, ffn_w2, ple_proj, ple_gate_down, ple_gate_up):
    bsz, seq, d = x.shape
    depth = p.shape[0]
    assert bsz == 1
    h = x.reshape(seq, d)
    p2 = p.reshape(depth, seq, p.shape[-1])
    gd_bf, gu_bf, proj_bf = (ple_gate_down.astype(BF16), ple_gate_up.astype(BF16),
                             ple_proj.astype(BF16))

    w_in, w_out = even_w_in[0].astype(BF16), even_w_out[0].astype(BF16)
    w1, w2 = ffn_w1[0].astype(BF16), ffn_w2[0].astype(BF16)

    hn = rmsnorm_bf16(h, norm_mix_g[0])
    out = None
    for i in range(depth):
        last = i == depth - 1
        nxt = i + 1
        if last:
            side_in = side_out = side_w1 = side_w2 = None
        else:
            stacks = (even_w_in, even_w_out) if nxt % 2 == 0 else (odd_w_in, odd_w_out)
            side_in, side_out = (stacks[0], nxt // 2), (stacks[1], nxt // 2)
            side_w1, side_w2 = (ffn_w1, nxt), (ffn_w2, nxt)
        if i % 2 == 0:
            e = i // 2
            z, w_in_next = matmul(hn, w_in, side=side_in, name="even_in")
            y = even_mixer(z, a_conv_w[e], a_conv_b[e], a_ln_g[e], a_ln_b[e],
                           b_ln_g[e], b_ln_b[e], b_ws[e], b_bs[e])
            h, w_out_next = matmul(y, w_out, resid=h, out_dtype=F32, side=side_out, name="even_out")
        else:
            o = i // 2
            z, w_in_next = matmul(hn, w_in, side=side_in, name="odd_in")
            y = odd_mixer(z, odd_conv_w[o])
            h, w_out_next = matmul(y, w_out, resid=h, out_dtype=F32, side=side_out, name="odd_out")
        hf = rmsnorm_bf16(h, norm_ffn_g[i])
        hid, w1_next = matmul(hf, w1, act="relu2", side=side_w1, name="ffn_up")
        h, w2_next = matmul(hid, w2, resid=h, out_dtype=F32, tk=4096, side=side_w2, name="ffn_down")
        g_next = final_norm_g if last else norm_mix_g[nxt]
        res = ple_layer(h, p2, i, norm_ple_g[i], gd_bf, gu_bf, proj_bf, g_next, last=last)
        if last:
            out = res
        else:
            h, hn = res
        w_in, w_out, w1, w2 = w_in_next, w_out_next, w1_next, w2_next
    return out.reshape(bsz, seq, d)
```

```python
import functools
import math
from typing import NamedTuple, Optional

import jax
import jax.numpy as jnp
from jax import lax
from jax.experimental import pallas as pl
from jax.experimental.pallas import tpu as pltpu

EPS = 1e-6
CHUNK = 128
HEAD = 128
LANES = 128
HALO = 32
SUBLANES = 8
BF16_ROWS = 16
ODD_GROUP_COLS = 256
GELU_C = math.sqrt(2.0 / math.pi)

V7X_VMEM_BYTES = 64 * 1024 * 1024
VMEM_LIMIT = V7X_VMEM_BYTES - 3 * 1024 * 1024

F32 = jnp.float32
BF16 = jnp.bfloat16


def _params(semantics):
    return pltpu.CompilerParams(dimension_semantics=semantics,
                                vmem_limit_bytes=VMEM_LIMIT)


def _rms_scale(x):
    return x * lax.rsqrt(jnp.mean(x * x, axis=-1, keepdims=True) + EPS)


def _layer_norm(x, g, b):
    mu = jnp.mean(x, axis=-1, keepdims=True)
    xc = x - mu
    var = jnp.mean(xc * xc, axis=-1, keepdims=True)
    return xc * lax.rsqrt(var + EPS) * g + b


def _gelu(x):
    return 0.5 * x * (1.0 + jnp.tanh(GELU_C * (x + 0.044715 * (x * x * x))))


def _rmsnorm_kernel(x_ref, g_ref, o_ref):
    o_ref[...] = (_rms_scale(x_ref[...]) * g_ref[...]).astype(o_ref.dtype)


def rmsnorm_bf16(x, g, *, tm=256):
    m, d = x.shape
    tm = min(tm, m)
    return pl.pallas_call(
        _rmsnorm_kernel,
        grid=(m // tm,),
        in_specs=[pl.BlockSpec((tm, d), lambda i: (i, 0)),
                  pl.BlockSpec((1, d), lambda i: (0, 0))],
        out_specs=pl.BlockSpec((tm, d), lambda i: (i, 0)),
        out_shape=jax.ShapeDtypeStruct((m, d), BF16),
        compiler_params=_params(("arbitrary",)),
        name="rmsnorm_bf16",
    )(x, g.reshape(1, d))


class Side(NamedTuple):
    stack: jax.Array
    layer: int
    scale: Optional[jax.Array] = None
    groups: int = 1


def _side_blocks(nsteps, rows):
    nblocks = 1
    while nblocks * 2 <= min(nsteps, rows // BF16_ROWS) and rows % (nblocks * 2) == 0:
        nblocks *= 2
    return nblocks


def _side_specs(sides, nsteps, step_of):
    in_specs, args, out_specs, out_shape, flags = [], [], [], [], []
    for side in sides:
        _, srows, scols = side.stack.shape
        nblocks = _side_blocks(nsteps, srows)
        brow = srows // nblocks
        layer = side.layer

        def blk(*ids, nblocks=nblocks):
            return jnp.minimum(step_of(*ids), nblocks - 1)

        in_specs.append(pl.BlockSpec((None, brow, scols),
                                     lambda *ids, blk=blk, layer=layer: (layer, blk(*ids), 0)))
        args.append(side.stack)
        if side.scale is not None:
            in_specs.append(pl.BlockSpec((brow, 1), lambda *ids, blk=blk: (blk(*ids), 0)))
            args.append(side.scale.reshape(srows, 1))
        out_specs.append(pl.BlockSpec((brow, scols), lambda *ids, blk=blk: (blk(*ids), 0)))
        out_shape.append(jax.ShapeDtypeStruct((srows, scols), BF16))
        flags.append((side.scale is not None, side.groups))
    return in_specs, args, out_specs, out_shape, tuple(flags)


def _cast_side(s_ref, sc_ref, so_ref, groups):
    if groups == 1:
        v = s_ref[...]
        if sc_ref is not None:
            v = v * sc_ref[...]
        so_ref[...] = v.astype(so_ref.dtype)
        return
    assert sc_ref is None
    gcols = s_ref.shape[1] // groups
    w = ODD_GROUP_COLS
    for jb in range(gcols // w):
        for g in range(groups):
            dst = (jb * groups + g) * w
            so_ref[:, dst:dst + w] = s_ref[:, g * gcols + jb * w:g * gcols + (jb + 1) * w].astype(so_ref.dtype)


def _take_sides(it, flags):
    return [(next(it), next(it) if has_scale else None, groups) for has_scale, groups in flags]


def _row_rms(x_ref):
    tm, d = x_ref.shape
    part = jnp.zeros((tm, LANES), F32)
    for c in range(d // LANES):
        v = x_ref[:, c * LANES:(c + 1) * LANES].astype(F32)
        part = part + v * v
    return lax.rsqrt(jnp.sum(part, axis=1, keepdims=True) * (1.0 / d) + EPS)


def _mm_kernel(*refs, nk, act, has_resid, emit_bf16, row_rms, side_flags):
    it = iter(refs)
    x_ref, w_ref = next(it), next(it)
    r_ref = next(it) if has_resid else None
    sides = _take_sides(it, side_flags)
    o_ref = next(it)
    obf_ref = next(it) if emit_bf16 else None
    for s_ref, sc_ref, groups in sides:
        _cast_side(s_ref, sc_ref, next(it), groups)
    if row_rms:
        rms_ref = next(it)

        @pl.when(pl.program_id(1) == 0)
        def _():
            rms_ref[...] = _row_rms(x_ref)

    def product():
        return jnp.dot(x_ref[...], w_ref[...], preferred_element_type=F32)

    def first(acc):
        if row_rms:
            acc = acc * rms_ref[...]
        if act == "relu2":
            acc = jnp.square(jnp.maximum(acc, 0.0))
        if has_resid:
            acc = r_ref[...] + acc
        return acc

    if nk == 1:
        res = first(product())
        o_ref[...] = res.astype(o_ref.dtype)
        if emit_bf16:
            obf_ref[...] = res.astype(obf_ref.dtype)
        return

    assert act is None and o_ref.dtype == F32 and not emit_bf16 and not row_rms
    k = pl.program_id(2)

    @pl.when(k == 0)
    def _():
        o_ref[...] = first(product())

    @pl.when(k > 0)
    def _():
        o_ref[...] += product()


def matmul(x, w, *, resid=None, act=None, out_dtype=BF16, emit_bf16=False, row_rms=False,
           tm=1024, tn=1024, tk=None, sides=(), name):
    m, kdim = x.shape
    n = w.shape[1]
    tm, tn = min(tm, m), min(tn, n)
    tk = kdim if tk is None else min(tk, kdim)
    nk = kdim // tk
    assert m % tm == 0 and n % tn == 0 and kdim % tk == 0
    assert not row_rms or nk == 1
    gi, gj = m // tm, n // tn
    in_specs = [pl.BlockSpec((tm, tk), lambda i, j, k: (i, k)),
                pl.BlockSpec((tk, tn), lambda i, j, k: (k, j))]
    args = [x, w]
    if resid is not None:
        in_specs.append(pl.BlockSpec((tm, tn), lambda i, j, k: (i, j)))
        args.append(resid)
    s_in, s_args, s_out, s_shape, side_flags = _side_specs(
        sides, gi * gj * nk, lambda i, j, k: (i * gj + j) * nk + k)
    out_specs = [pl.BlockSpec((tm, tn), lambda i, j, k: (i, j))]
    out_shape = [jax.ShapeDtypeStruct((m, n), out_dtype)]
    if emit_bf16:
        out_specs.append(pl.BlockSpec((tm, tn), lambda i, j, k: (i, j)))
        out_shape.append(jax.ShapeDtypeStruct((m, n), BF16))
    return pl.pallas_call(
        functools.partial(_mm_kernel, nk=nk, act=act, has_resid=resid is not None,
                          emit_bf16=emit_bf16, row_rms=row_rms, side_flags=side_flags),
        grid=(gi, gj, nk),
        in_specs=in_specs + s_in,
        out_specs=out_specs + s_out,
        out_shape=out_shape + s_shape,
        scratch_shapes=[pltpu.VMEM((tm, 1), F32)] if row_rms else [],
        compiler_params=_params(("arbitrary", "arbitrary", "arbitrary")),
        name=name,
    )(*args, *s_args)


def _odd_in_kernel(*refs, tm, width, side_flags):
    it = iter(refs)
    x_ref, w_ref, cw_ref = next(it), next(it), next(it)
    sides = _take_sides(it, side_flags)
    y_ref = next(it)
    for s_ref, sc_ref, groups in sides:
        _cast_side(s_ref, sc_ref, next(it), groups)
    carry_ref = next(it)

    i, j = pl.program_id(0), pl.program_id(1)

    @pl.when(i == 0)
    def _():
        carry_ref[j] = jnp.zeros(carry_ref.shape[1:], F32)

    acc = jnp.dot(x_ref[...], w_ref[...], preferred_element_type=F32)
    cg = acc.shape[1] // 3
    gate_b, gate_c, xv = acc[:, :cg], acc[:, cg:2 * cg], acc[:, 2 * cg:]
    g = gate_c * xv
    prev = carry_ref[j]
    carry_ref[j] = g[tm - SUBLANES:, :]
    full = jnp.concatenate([prev, g], axis=0)
    conv = cw_ref[width - 1:width, :] * g
    for k in range(width - 1):
        s = width - 1 - k
        conv = conv + cw_ref[k:k + 1, :] * full[SUBLANES - s:SUBLANES - s + tm, :]
    y_ref[...] = (gate_b * conv).astype(y_ref.dtype)


def odd_in_mixer(x, w_inter, conv_w, *, tm=1024, sides=(), name):
    m, kdim = x.shape
    dc = w_inter.shape[1] // 3
    cg = ODD_GROUP_COLS
    tm = min(tm, m)
    gi, gj = m // tm, dc // cg
    width = conv_w.shape[0]
    assert width - 1 <= SUBLANES and dc % cg == 0 and m % tm == 0
    wrows = -(-width // SUBLANES) * SUBLANES
    cw = jnp.concatenate([conv_w, jnp.zeros((wrows - width, dc), F32)], axis=0)
    s_in, s_args, s_out, s_shape, side_flags = _side_specs(sides, gi * gj, lambda i, j: i * gj + j)
    return pl.pallas_call(
        functools.partial(_odd_in_kernel, tm=tm, width=width, side_flags=side_flags),
        grid=(gi, gj),
        in_specs=[pl.BlockSpec((tm, kdim), lambda i, j: (i, 0)),
                  pl.BlockSpec((kdim, 3 * cg), lambda i, j: (0, j)),
                  pl.BlockSpec((wrows, cg), lambda i, j: (0, j))] + s_in,
        out_specs=[pl.BlockSpec((tm, cg), lambda i, j: (i, j))] + s_out,
        out_shape=[jax.ShapeDtypeStruct((m, dc), BF16)] + s_shape,
        scratch_shapes=[pltpu.VMEM((gj, SUBLANES, cg), F32)],
        compiler_params=_params(("arbitrary", "arbitrary")),
        name=name,
    )(x, w_inter, cw, *s_args)


def _even_mixer_kernel(aval_ref, agate_ref, u_ref, v_ref, hval_ref, hgate_ref,
                       cw_ref, cb_ref, alg_ref, alb_ref, blg_ref, blb_ref,
                       ws_ref, bias_ref, o_ref, glu_ref, conv_ref, vn_ref, *, tm, dc, width):
    i = pl.program_id(0)

    def glu(val_ref, gate_ref):
        return val_ref[...].astype(F32) * jax.nn.sigmoid(gate_ref[...].astype(F32))

    halo = glu(hval_ref, hgate_ref)
    glu_ref[0:HALO, :] = jnp.where(i > 0, halo, 0.0)
    glu_ref[HALO:, :] = glu(aval_ref, agate_ref)

    rb = 64
    nq = -(-width // SUBLANES)
    assert HALO >= SUBLANES * nq

    def conv_cols(c, carry):
        cs = pl.ds(pl.multiple_of(c * HEAD, HEAD), HEAD)
        wk = cw_ref[:, cs]
        for blk in range(tm // rb):
            r0 = HALO + blk * rb
            y = None
            for r in range(SUBLANES):
                b = None
                for q in range(nq):
                    s = SUBLANES * q + r
                    if s >= width:
                        continue
                    start = r0 - SUBLANES - SUBLANES * q
                    term = wk[s:s + 1, :] * glu_ref[start:start + rb + SUBLANES, cs]
                    b = term if b is None else b + term
                part = b[SUBLANES - r:SUBLANES - r + rb, :]
                y = part if y is None else y + part
            conv_ref[blk * rb:(blk + 1) * rb, cs] = y
        return carry

    lax.fori_loop(0, dc // HEAD, conv_cols, 0)

    a = conv_ref[...] + cb_ref[...]
    a = _layer_norm(a, alg_ref[...], alb_ref[...])
    o_ref[:, 0:dc] = (a * jax.nn.sigmoid(a)).astype(o_ref.dtype)

    gv = _gelu(v_ref[...].astype(F32))
    vn_ref[...] = _layer_norm(gv, blg_ref[...], blb_ref[...]).astype(vn_ref.dtype)

    nc = tm // CHUNK
    row = lax.broadcasted_iota(jnp.int32, (CHUNK, CHUNK), 0)
    col = lax.broadcasted_iota(jnp.int32, (CHUNK, CHUNK), 1)
    causal = row >= col
    for h in range(dc // HEAD):
        hs = slice(h * HEAD, (h + 1) * HEAD)
        ws = jnp.where(causal, ws_ref[h], 0.0).astype(BF16)
        vcat = jnp.concatenate(
            [vn_ref[c * CHUNK:(c + 1) * CHUNK, hs] for c in range(nc)], axis=1)
        sv = jnp.dot(ws, vcat, preferred_element_type=F32)
        bias = bias_ref[:, hs]
        for c in range(nc):
            rs = slice(c * CHUNK, (c + 1) * CHUNK)
            u = _gelu(u_ref[rs, hs].astype(F32))
            o_ref[rs, dc + h * HEAD: dc + (h + 1) * HEAD] = (
                u * (sv[:, c * HEAD:(c + 1) * HEAD] + bias)).astype(o_ref.dtype)


def even_mixer(z, conv_w, conv_b, a_ln_g, a_ln_b, b_ln_g, b_ln_b, b_ws, b_bs, *, tm=256):
    m = z.shape[0]
    dc = z.shape[1] // 4
    nh = dc // HEAD
    tm = min(tm, m)
    hb = tm // HALO
    width = conv_w.shape[0]
    wrows = -(-width // SUBLANES) * SUBLANES
    cw = jnp.concatenate([conv_w[::-1], jnp.zeros((wrows - width, dc), F32)], axis=0)
    bias = jnp.repeat(b_bs.T, HEAD, axis=1)
    row = lambda v: v.reshape(1, dc)

    def col_spec(cidx):
        return pl.BlockSpec((tm, dc), lambda i: (i, cidx))

    def halo_spec(cidx):
        return pl.BlockSpec((HALO, dc), lambda i: (jnp.maximum(i * hb - 1, 0), cidx))

    const = lambda shape: pl.BlockSpec(shape, lambda i: (0,) * len(shape))
    return pl.pallas_call(
        functools.partial(_even_mixer_kernel, tm=tm, dc=dc, width=width),
        grid=(m // tm,),
        in_specs=[col_spec(0), col_spec(1), col_spec(2), col_spec(3),
                  halo_spec(0), halo_spec(1),
                  const((wrows, dc)), const((1, dc)), const((1, dc)), const((1, dc)),
                  const((1, dc)), const((1, dc)),
                  const((nh, CHUNK, CHUNK)), const((CHUNK, dc))],
        out_specs=pl.BlockSpec((tm, 2 * dc), lambda i: (i, 0)),
        out_shape=jax.ShapeDtypeStruct((m, 2 * dc), BF16),
        scratch_shapes=[pltpu.VMEM((HALO + tm, dc), F32),
                        pltpu.VMEM((tm, dc), F32),
                        pltpu.VMEM((tm, dc), BF16)],
        compiler_params=_params(("arbitrary",)),
        name="even_mixer",
    )(z, z, z, z, z, z, cw, row(conv_b), row(a_ln_g), row(a_ln_b), row(b_ln_g), row(b_ln_b),
      b_ws, bias)


def _ple_kernel(h_ref, p_ref, gple_ref, gd_ref, gu_ref, proj_ref, gnext_ref, *out_refs, last):
    h = h_ref[...]
    hn = (_rms_scale(h) * gple_ref[...]).astype(BF16)
    t = jnp.dot(hn, gd_ref[...], preferred_element_type=F32)
    gate = jax.nn.sigmoid(jnp.dot(t.astype(BF16), gu_ref[...], preferred_element_type=F32))
    pp = jnp.dot(p_ref[...].astype(BF16), proj_ref[...], preferred_element_type=F32)
    hnew = h + gate * pp
    nxt = _rms_scale(hnew) * gnext_ref[...]
    if last:
        out_refs[0][...] = nxt
    else:
        out_refs[0][...] = hnew
        out_refs[1][...] = nxt.astype(BF16)


def ple_layer(h, p, layer, g_ple, gate_down, gate_up, proj, g_next, *, last, tm=256):
    m, d = h.shape
    pd = p.shape[-1]
    r = gate_down.shape[-1]
    tm = min(tm, m)
    row_spec = pl.BlockSpec((tm, d), lambda i: (i, 0))
    vec = lambda: pl.BlockSpec((1, d), lambda i: (0, 0))
    stacked = lambda a, b: pl.BlockSpec((None, a, b), lambda i: (layer, 0, 0))
    if last:
        out_shape = jax.ShapeDtypeStruct((m, d), F32)
        out_specs = row_spec
    else:
        out_shape = (jax.ShapeDtypeStruct((m, d), F32), jax.ShapeDtypeStruct((m, d), BF16))
        out_specs = (row_spec, row_spec)
    return pl.pallas_call(
        functools.partial(_ple_kernel, last=last),
        grid=(m // tm,),
        in_specs=[row_spec, pl.BlockSpec((None, tm, pd), lambda i: (layer, i, 0)), vec(),
                  stacked(d, r), stacked(r, d), stacked(pd, d), vec()],
        out_specs=out_specs,
        out_shape=out_shape,
        compiler_params=_params(("arbitrary",)),
        name="ple_last" if last else "ple",
    )(h, p, g_ple.reshape(1, d), gate_down, gate_up, proj, g_next.reshape(1, d))


def kernel(x, p, norm_mix_g, norm_ffn_g, norm_ple_g, final_norm_g, even_w_in, even_w_out, a_conv_w, a_conv_b, a_ln_g, a_ln_b, b_ln_g, b_ln_b, b_ws, b_bs, odd_w_in, odd_conv_w, odd_w_out, ffn_w1, ffn_w2, ple_proj, ple_gate_down, ple_gate_up):
    bsz, seq, d = x.shape
    depth = p.shape[0]
    assert bsz == 1
    h = x.reshape(seq, d)
    p2 = p.reshape(depth, seq, p.shape[-1])
    gd_bf, gu_bf, proj_bf = (ple_gate_down.astype(BF16), ple_gate_up.astype(BF16),
                             ple_proj.astype(BF16))

    def mix_sides(layer):
        if layer % 2 == 0:
            return [Side(even_w_in, layer // 2), Side(even_w_out, layer // 2)]
        return [Side(odd_w_in, layer // 2, groups=3), Side(odd_w_out, layer // 2)]

    def w1_side(layer):
        return Side(ffn_w1, layer, scale=norm_ffn_g[layer])

    w_in = even_w_in[0].astype(BF16)
    w_out = w1 = w2 = None

    hn = rmsnorm_bf16(h, norm_mix_g[0])
    out = None
    for i in range(depth):
        last = i == depth - 1
        nxt = i + 1
        sides_in = [] if last else mix_sides(nxt)
        if i == 0:
            sides_in = sides_in + [Side(even_w_out, 0), w1_side(0)]
        if i % 2 == 0:
            e = i // 2
            z, *cast = matmul(hn, w_in, sides=sides_in, name="even_in")
            y = even_mixer(z, a_conv_w[e], a_conv_b[e], a_ln_g[e], a_ln_b[e],
                           b_ln_g[e], b_ln_b[e], b_ws[e], b_bs[e])
        else:
            y, *cast = odd_in_mixer(hn, w_in, odd_conv_w[i // 2], sides=sides_in, name="odd_in")
        if not last:
            w_in_next, w_out_next = cast[0], cast[1]
            cast = cast[2:]
        if i == 0:
            w_out, w1 = cast

        h, hbf = matmul(y, w_out, resid=h, out_dtype=F32, emit_bf16=True,
                        name="even_out" if i % 2 == 0 else "odd_out")

        sides_up = ([] if last else [w1_side(nxt)]) + ([Side(ffn_w2, 0)] if i == 0 else [])
        hid, *cast = matmul(hbf, w1, act="relu2", row_rms=True, sides=sides_up, name="ffn_up")
        if not last:
            w1_next = cast[0]
            cast = cast[1:]
        if i == 0:
            w2, = cast

        h, *cast = matmul(hid, w2, resid=h, out_dtype=F32, tk=4096,
                          sides=[] if last else [Side(ffn_w2, nxt)], name="ffn_down")
        if not last:
            w2_next, = cast

        g_next = final_norm_g if last else norm_mix_g[nxt]
        res = ple_layer(h, p2, i, norm_ple_g[i], gd_bf, gu_bf, proj_bf, g_next, last=last)
        if last:
            out = res
        else:
            h, hn = res
            w_in, w_out, w1, w2 = w_in_next, w_out_next, w1_next, w2_next
    return out.reshape(bsz, seq, d)
```

```python
import functools
import math
from typing import NamedTuple, Optional

import jax
import jax.numpy as jnp
from jax import lax
from jax.experimental import pallas as pl
from jax.experimental.pallas import tpu as pltpu

EPS = 1e-6
CHUNK = 128
HEAD = 128
LANES = 128
HALO = 32
SUBLANES = 8
BF16_ROWS = 16
ODD_GROUP_COLS = 256
GELU_C = math.sqrt(2.0 / math.pi)

V7X_VMEM_BYTES = 64 * 1024 * 1024
VMEM_LIMIT = V7X_VMEM_BYTES - 3 * 1024 * 1024

F32 = jnp.float32
BF16 = jnp.bfloat16


def _params(semantics):
    return pltpu.CompilerParams(dimension_semantics=semantics,
                                vmem_limit_bytes=VMEM_LIMIT)


def _rms_scale(x):
    return x * lax.rsqrt(jnp.mean(x * x, axis=-1, keepdims=True) + EPS)


def _layer_norm(x, g, b):
    mu = jnp.mean(x, axis=-1, keepdims=True)
    xc = x - mu
    var = jnp.mean(xc * xc, axis=-1, keepdims=True)
    return xc * lax.rsqrt(var + EPS) * g + b


def _gelu(x):
    return 0.5 * x * (1.0 + jnp.tanh(GELU_C * (x + 0.044715 * (x * x * x))))


def _rmsnorm_kernel(x_ref, g_ref, o_ref):
    o_ref[...] = (_rms_scale(x_ref[...]) * g_ref[...]).astype(o_ref.dtype)


def rmsnorm_bf16(x, g, *, tm=256):
    m, d = x.shape
    tm = min(tm, m)
    return pl.pallas_call(
        _rmsnorm_kernel,
        grid=(m // tm,),
        in_specs=[pl.BlockSpec((tm, d), lambda i: (i, 0)),
                  pl.BlockSpec((1, d), lambda i: (0, 0))],
        out_specs=pl.BlockSpec((tm, d), lambda i: (i, 0)),
        out_shape=jax.ShapeDtypeStruct((m, d), BF16),
        compiler_params=_params(("arbitrary",)),
        name="rmsnorm_bf16",
    )(x, g.reshape(1, d))


class Side(NamedTuple):
    stack: jax.Array
    layer: int
    scale: Optional[jax.Array] = None
    groups: int = 1


def _side_blocks(nsteps, rows):
    nblocks = 1
    while nblocks * 2 <= min(nsteps, rows // BF16_ROWS) and rows % (nblocks * 2) == 0:
        nblocks *= 2
    return nblocks


def _side_specs(sides, nsteps, step_of):
    in_specs, args, out_specs, out_shape, flags = [], [], [], [], []
    for side in sides:
        _, srows, scols = side.stack.shape
        nblocks = _side_blocks(nsteps, srows)
        brow = srows // nblocks
        layer = side.layer

        def blk(*ids, nblocks=nblocks):
            return jnp.minimum(step_of(*ids), nblocks - 1)

        in_specs.append(pl.BlockSpec((None, brow, scols),
                                     lambda *ids, blk=blk, layer=layer: (layer, blk(*ids), 0)))
        args.append(side.stack)
        if side.scale is not None:
            in_specs.append(pl.BlockSpec((brow, 1), lambda *ids, blk=blk: (blk(*ids), 0)))
            args.append(side.scale.reshape(srows, 1))
        out_specs.append(pl.BlockSpec((brow, scols), lambda *ids, blk=blk: (blk(*ids), 0)))
        out_shape.append(jax.ShapeDtypeStruct((srows, scols), BF16))
        flags.append((side.scale is not None, side.groups))
    return in_specs, args, out_specs, out_shape, tuple(flags)


def _cast_side(s_ref, sc_ref, so_ref, groups):
    if groups == 1:
        v = s_ref[...]
        if sc_ref is not None:
            v = v * sc_ref[...]
        so_ref[...] = v.astype(so_ref.dtype)
        return
    assert sc_ref is None
    gcols = s_ref.shape[1] // groups
    w = ODD_GROUP_COLS
    for jb in range(gcols // w):
        for g in range(groups):
            dst = (jb * groups + g) * w
            so_ref[:, dst:dst + w] = s_ref[:, g * gcols + jb * w:g * gcols + (jb + 1) * w].astype(so_ref.dtype)


def _take_sides(it, flags):
    return [(next(it), next(it) if has_scale else None, groups) for has_scale, groups in flags]


def _row_rms(x_ref):
    tm, d = x_ref.shape
    part = jnp.zeros((tm, LANES), F32)
    for c in range(d // LANES):
        v = x_ref[:, c * LANES:(c + 1) * LANES].astype(F32)
        part = part + v * v
    return lax.rsqrt(jnp.sum(part, axis=1, keepdims=True) * (1.0 / d) + EPS)


def _mm_kernel(*refs, nk, act, has_resid, emit_bf16, row_rms, side_flags):
    it = iter(refs)
    x_ref, w_ref = next(it), next(it)
    r_ref = next(it) if has_resid else None
    sides = _take_sides(it, side_flags)
    o_ref = next(it)
    obf_ref = next(it) if emit_bf16 else None
    side_outs = [next(it) for _ in sides]

    def cast_sides():
        for (s_ref, sc_ref, groups), so_ref in zip(sides, side_outs):
            _cast_side(s_ref, sc_ref, so_ref, groups)

    if row_rms:
        rms_ref = next(it)

        @pl.when(pl.program_id(1) == 0)
        def _():
            rms_ref[...] = _row_rms(x_ref)

    def product():
        return jnp.dot(x_ref[...], w_ref[...], preferred_element_type=F32)

    def first(acc):
        if row_rms:
            acc = acc * rms_ref[...]
        if act == "relu2":
            acc = jnp.square(jnp.maximum(acc, 0.0))
        if has_resid:
            acc = r_ref[...] + acc
        return acc

    if nk == 1:
        res = first(product())
        o_ref[...] = res.astype(o_ref.dtype)
        if emit_bf16:
            obf_ref[...] = res.astype(obf_ref.dtype)
        cast_sides()
        return

    assert act is None and o_ref.dtype == F32 and not emit_bf16 and not row_rms
    k = pl.program_id(2)

    @pl.when(k == 0)
    def _():
        o_ref[...] = first(product())
        cast_sides()

    @pl.when(k > 0)
    def _():
        o_ref[...] += product()
        cast_sides()


def matmul(x, w, *, resid=None, act=None, out_dtype=BF16, emit_bf16=False, row_rms=False,
           tm=1024, tn=1024, tk=None, sides=(), name):
    m, kdim = x.shape
    n = w.shape[1]
    tm, tn = min(tm, m), min(tn, n)
    tk = kdim if tk is None else min(tk, kdim)
    nk = kdim // tk
    assert m % tm == 0 and n % tn == 0 and kdim % tk == 0
    assert not row_rms or nk == 1
    gi, gj = m // tm, n // tn
    in_specs = [pl.BlockSpec((tm, tk), lambda i, j, k: (i, k)),
                pl.BlockSpec((tk, tn), lambda i, j, k: (k, j))]
    args = [x, w]
    if resid is not None:
        in_specs.append(pl.BlockSpec((tm, tn), lambda i, j, k: (i, j)))
        args.append(resid)
    s_in, s_args, s_out, s_shape, side_flags = _side_specs(
        sides, gi * gj * nk, lambda i, j, k: (i * gj + j) * nk + k)
    out_specs = [pl.BlockSpec((tm, tn), lambda i, j, k: (i, j))]
    out_shape = [jax.ShapeDtypeStruct((m, n), out_dtype)]
    if emit_bf16:
        out_specs.append(pl.BlockSpec((tm, tn), lambda i, j, k: (i, j)))
        out_shape.append(jax.ShapeDtypeStruct((m, n), BF16))
    return pl.pallas_call(
        functools.partial(_mm_kernel, nk=nk, act=act, has_resid=resid is not None,
                          emit_bf16=emit_bf16, row_rms=row_rms, side_flags=side_flags),
        grid=(gi, gj, nk),
        in_specs=in_specs + s_in,
        out_specs=out_specs + s_out,
        out_shape=out_shape + s_shape,
        scratch_shapes=[pltpu.VMEM((tm, 1), F32)] if row_rms else [],
        compiler_params=_params(("arbitrary", "arbitrary", "arbitrary")),
        name=name,
    )(*args, *s_args)


def _odd_in_kernel(*refs, tm, width, side_flags):
    it = iter(refs)
    x_ref, w_ref, cw_ref = next(it), next(it), next(it)
    sides = _take_sides(it, side_flags)
    y_ref = next(it)
    side_outs = [next(it) for _ in sides]
    carry_ref = next(it)

    i, j = pl.program_id(0), pl.program_id(1)

    @pl.when(i == 0)
    def _():
        carry_ref[j] = jnp.zeros(carry_ref.shape[1:], F32)

    acc = jnp.dot(x_ref[...], w_ref[...], preferred_element_type=F32)
    cg = acc.shape[1] // 3
    gate_b, gate_c, xv = acc[:, :cg], acc[:, cg:2 * cg], acc[:, 2 * cg:]
    g = gate_c * xv
    prev = carry_ref[j]
    carry_ref[j] = g[tm - SUBLANES:, :]
    full = jnp.concatenate([prev, g], axis=0)
    conv = cw_ref[width - 1:width, :] * g
    for k in range(width - 1):
        s = width - 1 - k
        conv = conv + cw_ref[k:k + 1, :] * full[SUBLANES - s:SUBLANES - s + tm, :]
    y_ref[...] = (gate_b * conv).astype(y_ref.dtype)
    for (s_ref, sc_ref, groups), so_ref in zip(sides, side_outs):
        _cast_side(s_ref, sc_ref, so_ref, groups)


def odd_in_mixer(x, w_inter, conv_w, *, tm=1024, sides=(), name):
    m, kdim = x.shape
    dc = w_inter.shape[1] // 3
    cg = ODD_GROUP_COLS
    tm = min(tm, m)
    gi, gj = m // tm, dc // cg
    width = conv_w.shape[0]
    assert width - 1 <= SUBLANES and dc % cg == 0 and m % tm == 0
    wrows = -(-width // SUBLANES) * SUBLANES
    cw = jnp.concatenate([conv_w, jnp.zeros((wrows - width, dc), F32)], axis=0)
    s_in, s_args, s_out, s_shape, side_flags = _side_specs(sides, gi * gj, lambda i, j: i * gj + j)
    return pl.pallas_call(
        functools.partial(_odd_in_kernel, tm=tm, width=width, side_flags=side_flags),
        grid=(gi, gj),
        in_specs=[pl.BlockSpec((tm, kdim), lambda i, j: (i, 0)),
                  pl.BlockSpec((kdim, 3 * cg), lambda i, j: (0, j)),
                  pl.BlockSpec((wrows, cg), lambda i, j: (0, j))] + s_in,
        out_specs=[pl.BlockSpec((tm, cg), lambda i, j: (i, j))] + s_out,
        out_shape=[jax.ShapeDtypeStruct((m, dc), BF16)] + s_shape,
        scratch_shapes=[pltpu.VMEM((gj, SUBLANES, cg), F32)],
        compiler_params=_params(("arbitrary", "arbitrary")),
        name=name,
    )(x, w_inter, cw, *s_args)


def _even_mixer_kernel(aval_ref, agate_ref, u_ref, v_ref, hval_ref, hgate_ref,
                       cw_ref, cb_ref, alg_ref, alb_ref, blg_ref, blb_ref,
                       ws_ref, bias_ref, o_ref, glu_ref, conv_ref, vn_ref, *, tm, dc, width):
    i = pl.program_id(0)

    def glu(val_ref, gate_ref):
        return val_ref[...].astype(F32) * jax.nn.sigmoid(gate_ref[...].astype(F32))

    halo = glu(hval_ref, hgate_ref)
    glu_ref[0:HALO, :] = jnp.where(i > 0, halo, 0.0)
    glu_ref[HALO:, :] = glu(aval_ref, agate_ref)

    rb = 64
    nq = -(-width // SUBLANES)
    assert HALO >= SUBLANES * nq

    def conv_cols(c, carry):
        cs = pl.ds(pl.multiple_of(c * HEAD, HEAD), HEAD)
        wk = cw_ref[:, cs]
        for blk in range(tm // rb):
            r0 = HALO + blk * rb
            y = None
            for r in range(SUBLANES):
                b = None
                for q in range(nq):
                    s = SUBLANES * q + r
                    if s >= width:
                        continue
                    start = r0 - SUBLANES - SUBLANES * q
                    term = wk[s:s + 1, :] * glu_ref[start:start + rb + SUBLANES, cs]
                    b = term if b is None else b + term
                part = b[SUBLANES - r:SUBLANES - r + rb, :]
                y = part if y is None else y + part
            conv_ref[blk * rb:(blk + 1) * rb, cs] = y
        return carry

    lax.fori_loop(0, dc // HEAD, conv_cols, 0)

    a = conv_ref[...] + cb_ref[...]
    a = _layer_norm(a, alg_ref[...], alb_ref[...])
    o_ref[:, 0:dc] = (a * jax.nn.sigmoid(a)).astype(o_ref.dtype)

    gv = _gelu(v_ref[...].astype(F32))
    vn_ref[...] = _layer_norm(gv, blg_ref[...], blb_ref[...]).astype(vn_ref.dtype)

    nc = tm // CHUNK
    row = lax.broadcasted_iota(jnp.int32, (CHUNK, CHUNK), 0)
    col = lax.broadcasted_iota(jnp.int32, (CHUNK, CHUNK), 1)
    causal = row >= col
    for h in range(dc // HEAD):
        hs = slice(h * HEAD, (h + 1) * HEAD)
        ws = jnp.where(causal, ws_ref[h], 0.0).astype(BF16)
        vcat = jnp.concatenate(
            [vn_ref[c * CHUNK:(c + 1) * CHUNK, hs] for c in range(nc)], axis=1)
        sv = jnp.dot(ws, vcat, preferred_element_type=F32)
        bias = bias_ref[:, hs]
        for c in range(nc):
            rs = slice(c * CHUNK, (c + 1) * CHUNK)
            u = _gelu(u_ref[rs, hs].astype(F32))
            o_ref[rs, dc + h * HEAD: dc + (h + 1) * HEAD] = (
                u * (sv[:, c * HEAD:(c + 1) * HEAD] + bias)).astype(o_ref.dtype)


def even_mixer(z, conv_w, conv_b, a_ln_g, a_ln_b, b_ln_g, b_ln_b, b_ws, b_bs, *, tm=256):
    m = z.shape[0]
    dc = z.shape[1] // 4
    nh = dc // HEAD
    tm = min(tm, m)
    hb = tm // HALO
    width = conv_w.shape[0]
    wrows = -(-width // SUBLANES) * SUBLANES
    cw = jnp.concatenate([conv_w[::-1], jnp.zeros((wrows - width, dc), F32)], axis=0)
    bias = jnp.repeat(b_bs.T, HEAD, axis=1)
    row = lambda v: v.reshape(1, dc)

    def col_spec(cidx):
        return pl.BlockSpec((tm, dc), lambda i: (i, cidx))

    def halo_spec(cidx):
        return pl.BlockSpec((HALO, dc), lambda i: (jnp.maximum(i * hb - 1, 0), cidx))

    const = lambda shape: pl.BlockSpec(shape, lambda i: (0,) * len(shape))
    return pl.pallas_call(
        functools.partial(_even_mixer_kernel, tm=tm, dc=dc, width=width),
        grid=(m // tm,),
        in_specs=[col_spec(0), col_spec(1), col_spec(2), col_spec(3),
                  halo_spec(0), halo_spec(1),
                  const((wrows, dc)), const((1, dc)), const((1, dc)), const((1, dc)),
                  const((1, dc)), const((1, dc)),
                  const((nh, CHUNK, CHUNK)), const((CHUNK, dc))],
        out_specs=pl.BlockSpec((tm, 2 * dc), lambda i: (i, 0)),
        out_shape=jax.ShapeDtypeStruct((m, 2 * dc), BF16),
        scratch_shapes=[pltpu.VMEM((HALO + tm, dc), F32),
                        pltpu.VMEM((tm, dc), F32),
                        pltpu.VMEM((tm, dc), BF16)],
        compiler_params=_params(("arbitrary",)),
        name="even_mixer",
    )(z, z, z, z, z, z, cw, row(conv_b), row(a_ln_g), row(a_ln_b), row(b_ln_g), row(b_ln_b),
      b_ws, bias)


def _ple_kernel(h_ref, p_ref, gple_ref, gd_ref, gu_ref, proj_ref, gnext_ref, *out_refs, last):
    h = h_ref[...]
    hn = (_rms_scale(h) * gple_ref[...]).astype(BF16)
    t = jnp.dot(hn, gd_ref[...], preferred_element_type=F32)
    gate = jax.nn.sigmoid(jnp.dot(t.astype(BF16), gu_ref[...], preferred_element_type=F32))
    pp = jnp.dot(p_ref[...].astype(BF16), proj_ref[...], preferred_element_type=F32)
    hnew = h + gate * pp
    nxt = _rms_scale(hnew) * gnext_ref[...]
    if last:
        out_refs[0][...] = nxt
    else:
        out_refs[0][...] = hnew
        out_refs[1][...] = nxt.astype(BF16)


def ple_layer(h, p, layer, g_ple, gate_down, gate_up, proj, g_next, *, last, tm=256):
    m, d = h.shape
    pd = p.shape[-1]
    r = gate_down.shape[-1]
    tm = min(tm, m)
    row_spec = pl.BlockSpec((tm, d), lambda i: (i, 0))
    vec = lambda: pl.BlockSpec((1, d), lambda i: (0, 0))
    stacked = lambda a, b: pl.BlockSpec((None, a, b), lambda i: (layer, 0, 0))
    if last:
        out_shape = jax.ShapeDtypeStruct((m, d), F32)
        out_specs = row_spec
    else:
        out_shape = (jax.ShapeDtypeStruct((m, d), F32), jax.ShapeDtypeStruct((m, d), BF16))
        out_specs = (row_spec, row_spec)
    return pl.pallas_call(
        functools.partial(_ple_kernel, last=last),
        grid=(m // tm,),
        in_specs=[row_spec, pl.BlockSpec((None, tm, pd), lambda i: (layer, i, 0)), vec(),
                  stacked(d, r), stacked(r, d), stacked(pd, d), vec()],
        out_specs=out_specs,
        out_shape=out_shape,
        compiler_params=_params(("arbitrary",)),
        name="ple_last" if last else "ple",
    )(h, p, g_ple.reshape(1, d), gate_down, gate_up, proj, g_next.reshape(1, d))


def kernel(x, p, norm_mix_g, norm_ffn_g, norm_ple_g, final_norm_g, even_w_in, even_w_out, a_conv_w, a_conv_b, a_ln_g, a_ln_b, b_ln_g, b_ln_b, b_ws, b_bs, odd_w_in, odd_conv_w, odd_w_out, ffn_w1, ffn_w2, ple_proj, ple_gate_down, ple_gate_up):
    bsz, seq, d = x.shape
    depth = p.shape[0]
    assert bsz == 1
    h = x.reshape(seq, d)
    p2 = p.reshape(depth, seq, p.shape[-1])
    gd_bf, gu_bf, proj_bf = (ple_gate_down.astype(BF16), ple_gate_up.astype(BF16),
                             ple_proj.astype(BF16))

    def mix_sides(layer):
        if layer % 2 == 0:
            return [Side(even_w_in, layer // 2), Side(even_w_out, layer // 2)]
        return [Side(odd_w_in, layer // 2, groups=3), Side(odd_w_out, layer // 2)]

    def w1_side(layer):
        return Side(ffn_w1, layer, scale=norm_ffn_g[layer])

    w_in = even_w_in[0].astype(BF16)
    w_out = w1 = w2 = None

    hn = rmsnorm_bf16(h, norm_mix_g[0])
    out = None
    for i in range(depth):
        last = i == depth - 1
        nxt = i + 1
        sides_in = [] if last else mix_sides(nxt)
        if i == 0:
            sides_in = sides_in + [Side(even_w_out, 0), w1_side(0)]
        if i % 2 == 0:
            e = i // 2
            z, *cast = matmul(hn, w_in, sides=sides_in, name="even_in")
            y = even_mixer(z, a_conv_w[e], a_conv_b[e], a_ln_g[e], a_ln_b[e],
                           b_ln_g[e], b_ln_b[e], b_ws[e], b_bs[e])
        else:
            y, *cast = odd_in_mixer(hn, w_in, odd_conv_w[i // 2], sides=sides_in, name="odd_in")
        if not last:
            w_in_next, w_out_next = cast[0], cast[1]
            cast = cast[2:]
        if i == 0:
            w_out, w1 = cast

        h, hbf = matmul(y, w_out, resid=h, out_dtype=F32, emit_bf16=True,
                        name="even_out" if i % 2 == 0 else "odd_out")

        sides_up = ([] if last else [w1_side(nxt)]) + ([Side(ffn_w2, 0)] if i == 0 else [])
        hid, *cast = matmul(hbf, w1, act="relu2", row_rms=True, sides=sides_up, name="ffn_up")
        if not last:
            w1_next = cast[0]
            cast = cast[1:]
        if i == 0:
            w2, = cast

        h, *cast = matmul(hid, w2, resid=h, out_dtype=F32, tk=4096,
                          sides=[] if last else [Side(ffn_w2, nxt)], name="ffn_down")
        if not last:
            w2_next, = cast

        g_next = final_norm_g if last else norm_mix_g[nxt]
        res = ple_layer(h, p2, i, norm_ple_g[i], gd_bf, gu_bf, proj_bf, g_next, last=last)
        if last:
            out = res
        else:
            h, hn = res
            w_in, w_out, w1, w2 = w_in_next, w_out_next, w1_next, w2_next
    return out.reshape(bsz, seq, d)
```
